```python
import jax, jax.numpy as jnp
from jax import lax
import numpy as np

D_MODEL = 2048
BATCH = 4
SEQ = 2048
DEPTH = 1

MIX_WIDTH = D_MODEL
RET_WIDTH = D_MODEL // 2
POOL_WIDTH = MIX_WIDTH - RET_WIDTH
RET_HEADS = 8
RET_HEAD_DIM = RET_WIDTH // RET_HEADS
CHUNK = 128
ROPE_BASE = 10000.0
POOL_WINDOWS = (2, 4, 8, 16)
POOL_GROUP = POOL_WIDTH // len(POOL_WINDOWS)
IN_WIDTH = 4 * RET_WIDTH + POOL_WIDTH
N_EXPERTS = 32
TOP_K = 4
D_FF = D_MODEL
SWIGLU_LIMIT = 7.0
SWIGLU_ALPHA = 1.702
LN_EPS = 1e-5
GN_EPS = 1e-6
DEEPNORM_ALPHA = (2.0 * DEPTH) ** 0.25
DEEPNORM_BETA = (8.0 * DEPTH) ** -0.25

kernel_name = "hybrid_retention_pool_moe_deepnorm"


def layer_norm(x, g, b):
    xf = x.astype(jnp.float32)
    mu = jnp.mean(xf, axis=-1, keepdims=True)
    xc = xf - mu
    var = jnp.mean(xc * xc, axis=-1, keepdims=True)
    y = xc * lax.rsqrt(var + LN_EPS) * g.astype(jnp.float32) + b.astype(jnp.float32)
    return y.astype(x.dtype)


def rotary(t, pos):
    half = t.shape[-1] // 2
    inv = ROPE_BASE ** (-jnp.arange(half, dtype=jnp.float32) / half)
    ang = pos.astype(jnp.float32)[:, None] * inv[None, :]
    cos = jnp.cos(ang)[None, :, None, :]
    sin = jnp.sin(ang)[None, :, None, :]
    t1, t2 = t[..., :half], t[..., half:]
    return jnp.concatenate([t1 * cos - t2 * sin, t1 * sin + t2 * cos], axis=-1)


def retention_chunkwise(q, k, v):
    B, S, H, dh = q.shape
    nc = S // CHUNK
    log_g = jnp.log(1.0 - 2.0 ** (-5.0 - jnp.arange(H, dtype=jnp.float32)))
    def to_chunks(t):
        return t.reshape(B, nc, CHUNK, H, dh).transpose(0, 3, 1, 2, 4)
    qc, kc, vc = to_chunks(q), to_chunks(k), to_chunks(v)
    idx = jnp.arange(CHUNK, dtype=jnp.float32)
    diff = idx[:, None] - idx[None, :]
    decay = jnp.where(diff[None] >= 0, jnp.exp(diff[None] * log_g[:, None, None]), 0.0)
    scores = jnp.einsum('bhncd,bhnmd->bhncm', qc, kc) * decay[None, :, None]
    y_in = jnp.einsum('bhncm,bhnme->bhnce', scores, vc)
    zeta = jnp.exp((CHUNK - 1.0 - idx)[None, :] * log_g[:, None])
    kv = jnp.einsum('bhncd,hc,bhnce->nbhde', kc, zeta, vc)
    g_chunk = jnp.exp(CHUNK * log_g)[None, :, None, None]
    def step(R, kv_n):
        return R * g_chunk + kv_n, R
    _, r_prev = lax.scan(step, jnp.zeros_like(kv[0]), kv)
    xi = jnp.exp((idx + 1.0)[None, :] * log_g[:, None])
    y_cross = jnp.einsum('bhncd,hc,nbhde->bhnce', qc, xi, r_prev)
    y = y_in + y_cross
    return y.transpose(0, 2, 3, 1, 4).reshape(B, S, H, dh)


def multiscale_pool(u, pool_w, pool_scale):
    B, S, _ = u.shape
    uf = u.astype(jnp.float32)
    cs = jnp.concatenate([jnp.zeros((B, 1, POOL_WIDTH), jnp.float32), jnp.cumsum(uf, axis=1)], axis=1)
    t = jnp.arange(S)
    outs = []
    for gi, w in enumerate(POOL_WINDOWS):
        lo_c, hi_c = gi * POOL_GROUP, (gi + 1) * POOL_GROUP
        c = cs[:, :, lo_c:hi_c]
        lower = jnp.concatenate([jnp.zeros((B, w - 1, POOL_GROUP), jnp.float32), c[:, :S + 1 - w]], axis=1)
        cnt = jnp.minimum(t + 1, w).astype(jnp.float32)[None, :, None]
        outs.append((c[:, 1:] - lower) / cnt - uf[:, :, lo_c:hi_c])
    p = jnp.stack(outs, axis=2)
    p = jnp.einsum('bsgc,gcd->bsgd', p, pool_w.astype(jnp.float32))
    return (p.reshape(B, S, POOL_WIDTH) * pool_scale.astype(jnp.float32)).astype(u.dtype)


def token_mixer(x, w_in, gn_gain, pool_w, pool_scale, w_out):
    B, S, _ = x.shape
    proj = x @ w_in
    q, k, v, g, u = jnp.split(proj, [RET_WIDTH, 2 * RET_WIDTH, 3 * RET_WIDTH, 4 * RET_WIDTH], axis=-1)
    pos = jnp.arange(S)
    shp = (B, S, RET_HEADS, RET_HEAD_DIM)
    qf = rotary(q.astype(jnp.float32).reshape(shp), pos)
    kf = rotary(k.astype(jnp.float32).reshape(shp), pos) * (RET_HEAD_DIM ** -0.5)
    vf = v.astype(jnp.float32).reshape(shp)
    y = retention_chunkwise(qf, kf, vf)
    mu = jnp.mean(y, axis=-1, keepdims=True)
    yc = y - mu
    y = yc * lax.rsqrt(jnp.mean(yc * yc, axis=-1, keepdims=True) + GN_EPS)
    y = y.reshape(B, S, RET_WIDTH) * gn_gain.astype(jnp.float32)
    ret_out = (jax.nn.silu(g.astype(jnp.float32)) * y).astype(x.dtype)
    pool_out = multiscale_pool(u, pool_w, pool_scale)
    mixed = jnp.concatenate([ret_out, pool_out], axis=-1)
    return mixed @ w_out


def moe_ffn(x, router_w, router_b, w_gate_up, b_gate_up, w_down, b_down):
    B, S, D = x.shape
    T = B * S
    h = x.reshape(T, D)
    logits = (h @ router_w).astype(jnp.float32) + router_b.astype(jnp.float32)
    top_vals, top_idx = lax.top_k(logits, TOP_K)
    gates = jax.nn.softmax(top_vals, axis=-1)
    flat_e = top_idx.reshape(-1)
    order = jnp.argsort(flat_e)
    tok = order // TOP_K
    e_sorted = flat_e[order]
    group_sizes = jnp.bincount(flat_e, length=N_EXPERTS).astype(jnp.int32)
    xs = h[tok]
    hgu = lax.ragged_dot(xs, w_gate_up, group_sizes) + b_gate_up[e_sorted]
    hgu = hgu.astype(jnp.float32)
    x_glu = jnp.minimum(hgu[:, 0::2], SWIGLU_LIMIT)
    x_lin = jnp.clip(hgu[:, 1::2], -SWIGLU_LIMIT, SWIGLU_LIMIT)
    act = (x_glu * jax.nn.sigmoid(SWIGLU_ALPHA * x_glu) * (x_lin + 1.0)).astype(x.dtype)
    out = lax.ragged_dot(act, w_down, group_sizes) + b_down[e_sorted]
    out = out * gates.reshape(-1)[order][:, None].astype(out.dtype)
    y = jnp.zeros((T, D), out.dtype).at[tok].add(out)
    return y.reshape(B, S, D)


def setup_inputs(seed: int = 0) -> dict:
    key = jax.random.key(seed)
    ks = jax.random.split(key, 16)
    f32 = jnp.float32
    nrm = lambda k, s: jax.random.normal(k, s, f32)
    return {
        "x": nrm(ks[0], (BATCH, SEQ, D_MODEL)),
        "w_in": nrm(ks[1], (DEPTH, D_MODEL, IN_WIDTH)) * D_MODEL ** -0.5,
        "ret_gn_gain": 1.0 + 0.02 * nrm(ks[2], (DEPTH, RET_WIDTH)),
        "pool_w": nrm(ks[3], (DEPTH, len(POOL_WINDOWS), POOL_GROUP, POOL_GROUP)) * POOL_GROUP ** -0.5,
        "pool_scale": 1.0 + 0.02 * nrm(ks[4], (DEPTH, POOL_WIDTH)),
        "w_out": nrm(ks[5], (DEPTH, MIX_WIDTH, D_MODEL)) * (MIX_WIDTH ** -0.5 * DEEPNORM_BETA),
        "ln1_g": 1.0 + 0.02 * nrm(ks[6], (DEPTH, D_MODEL)),
        "ln1_b": 0.02 * nrm(ks[7], (DEPTH, D_MODEL)),
        "router_w": nrm(ks[8], (DEPTH, D_MODEL, N_EXPERTS)) * D_MODEL ** -0.5,
        "router_b": 0.01 * nrm(ks[9], (DEPTH, N_EXPERTS)),
        "w_gate_up": nrm(ks[10], (DEPTH, N_EXPERTS, D_MODEL, 2 * D_FF)) * D_MODEL ** -0.5,
        "b_gate_up": 0.01 * nrm(ks[11], (DEPTH, N_EXPERTS, 2 * D_FF)),
        "w_down": nrm(ks[12], (DEPTH, N_EXPERTS, D_FF, D_MODEL)) * (D_FF ** -0.5 * DEEPNORM_BETA),
        "b_down": 0.01 * nrm(ks[13], (DEPTH, N_EXPERTS, D_MODEL)),
        "ln2_g": 1.0 + 0.02 * nrm(ks[14], (DEPTH, D_MODEL)),
        "ln2_b": 0.02 * nrm(ks[15], (DEPTH, D_MODEL)),
    }


def reference(x, w_in, ret_gn_gain, pool_w, pool_scale, w_out, ln1_g, ln1_b,
              router_w, router_b, w_gate_up, b_gate_up, w_down, b_down, ln2_g, ln2_b):
    h = x
    for l in range(DEPTH):
        mix = token_mixer(h, w_in[l], ret_gn_gain[l], pool_w[l], pool_scale[l], w_out[l])
        h = layer_norm(DEEPNORM_ALPHA * h + mix, ln1_g[l], ln1_b[l])
        ffn = moe_ffn(h, router_w[l], router_b[l], w_gate_up[l], b_gate_up[l], w_down[l], b_down[l])
        h = layer_norm(DEEPNORM_ALPHA * h + ffn, ln2_g[l], ln2_b[l])
    return h
```

```python
import functools

import jax
import jax.numpy as jnp
from jax import lax
from jax.experimental import pallas as pl
from jax.experimental.pallas import tpu as pltpu

RET_HEADS = 8
HEAD_DIM = 128
CHUNK = 128
ROPE_BASE = 10000.0
POOL_WINDOWS = (2, 4, 8, 16)
MAX_WINDOW = max(POOL_WINDOWS)
N_EXPERTS = 32
TOP_K = 4
SWIGLU_LIMIT = 7.0
SWIGLU_ALPHA = 1.702
LN_EPS = 1e-5
GN_EPS = 1e-6
DEPTH = 1
DEEPNORM_ALPHA = (2.0 * DEPTH) ** 0.25

VMEM_LIMIT_BYTES = 56 * 1024 * 1024

INPROJ_TM = 512
INPROJ_TN = 2560
OUTPROJ_TM = 512
MOE_SUB = 256
MOE_RMAX = 1280
MOE_NSUB = MOE_RMAX // MOE_SUB
MOE_FU = 256
COMBINE_TM = 128
SLOTS_CHUNKS = 16

F32 = jnp.float32
BF16 = jnp.bfloat16


def _inproj_kernel(x_ref, w_ref, o_ref):
    o_ref[...] = jnp.dot(x_ref[...].astype(BF16), w_ref[...],
                         preferred_element_type=F32).astype(o_ref.dtype)


def _in_proj(x2d, w_bf16):
    t, d = x2d.shape
    n = w_bf16.shape[1]
    tm, tn = INPROJ_TM, INPROJ_TN
    return pl.pallas_call(
        _inproj_kernel,
        out_shape=jax.ShapeDtypeStruct((t, n), BF16),
        grid=(n // tn, t // tm),
        in_specs=[pl.BlockSpec((tm, d), lambda j, i: (i, 0)),
                  pl.BlockSpec((d, tn), lambda j, i: (0, j))],
        out_specs=pl.BlockSpec((tm, tn), lambda j, i: (i, j)),
        compiler_params=pltpu.CompilerParams(
            dimension_semantics=("arbitrary", "arbitrary"),
            vmem_limit_bytes=VMEM_LIMIT_BYTES),
        name="in_proj",
    )(x2d, w_bf16)


def _mixer_kernel(gch_ref, proj_ref, cq_ref, sq_ref, ck_ref, sk_ref, decay_ref, xi_ref, zeta_ref,
                  gain_ref, poolw_ref, pscale_ref, o_ref, r_scr, tail_scr, *, ret_width, pool_group):
    n = pl.program_id(1)

    @pl.when(n == 0)
    def _():
        r_scr[...] = jnp.zeros_like(r_scr)
        tail_scr[...] = jnp.zeros_like(tail_scr)

    cq, sq, ck, sk = cq_ref[...], sq_ref[...], ck_ref[...], sk_ref[...]
    half = HEAD_DIM // 2
    for h in range(RET_HEADS):
        c0 = h * HEAD_DIM
        q = proj_ref[:, c0:c0 + HEAD_DIM].astype(F32)
        k = proj_ref[:, ret_width + c0:ret_width + c0 + HEAD_DIM].astype(F32)
        v = proj_ref[:, 2 * ret_width + c0:2 * ret_width + c0 + HEAD_DIM]
        g = proj_ref[:, 3 * ret_width + c0:3 * ret_width + c0 + HEAD_DIM].astype(F32)
        qr = q * cq + pltpu.roll(q, half, 1) * sq
        kr = k * ck + pltpu.roll(k, half, 1) * sk
        qb = qr.astype(BF16)
        kb = kr.astype(BF16)
        s = lax.dot_general(qb, kb, (((1,), (1,)), ((), ())), preferred_element_type=F32) * decay_ref[h]
        r = r_scr[h]
        y = jnp.dot(s.astype(BF16), v, preferred_element_type=F32)
        y = y + xi_ref[h] * jnp.dot(qb, r.astype(BF16), preferred_element_type=F32)
        kz = (kr * zeta_ref[h]).astype(BF16)
        kv = lax.dot_general(kz, v, (((0,), (0,)), ((), ())), preferred_element_type=F32)
        r_scr[h] = r * gch_ref[h] + kv
        mu = jnp.mean(y, axis=-1, keepdims=True)
        yc = y - mu
        var = jnp.mean(yc * yc, axis=-1, keepdims=True)
        yn = yc * lax.rsqrt(var + GN_EPS) * gain_ref[:, c0:c0 + HEAD_DIM]
        o_ref[:, c0:c0 + HEAD_DIM] = (g * jax.nn.sigmoid(g) * yn).astype(o_ref.dtype)

    u0 = 4 * ret_width
    t = n * CHUNK + lax.broadcasted_iota(jnp.int32, (CHUNK, 1), 0)
    for gi, w in enumerate(POOL_WINDOWS):
        lo = gi * pool_group
        u = proj_ref[:, u0 + lo:u0 + lo + pool_group].astype(F32)
        s = jnp.concatenate([tail_scr[:, lo:lo + pool_group], u], axis=0)
        sh = 1
        while sh < w:
            s = s + pltpu.roll(s, sh, 0)
            sh *= 2
        cnt = jnp.minimum(t + 1, w).astype(F32)
        p = s[MAX_WINDOW:] / cnt - u
        pm = jnp.dot(p.astype(BF16), poolw_ref[gi], preferred_element_type=F32)
        pm = pm * pscale_ref[:, lo:lo + pool_group]
        o_ref[:, ret_width + lo:ret_width + lo + pool_group] = pm.astype(o_ref.dtype)
        tail_scr[:, lo:lo + pool_group] = u[CHUNK - MAX_WINDOW:]


def _mixer(proj3, tables, gn_gain, pool_w_bf16, pool_scale, ret_width, pool_width):
    b, s, in_width = proj3.shape
    cq, sq, ck, sk, decay, xi, zeta, gch = tables
    pool_group = pool_width // len(POOL_WINDOWS)
    mix_width = ret_width + pool_width
    tab_spec = pl.BlockSpec((CHUNK, HEAD_DIM), lambda bi, n, g: (n, 0))
    head_spec = pl.BlockSpec((RET_HEADS, CHUNK, HEAD_DIM), lambda bi, n, g: (0, 0, 0))
    kern = functools.partial(_mixer_kernel, ret_width=ret_width, pool_group=pool_group)
    return pl.pallas_call(
        kern,
        out_shape=jax.ShapeDtypeStruct((b, s, mix_width), BF16),
        grid_spec=pltpu.PrefetchScalarGridSpec(
            num_scalar_prefetch=1,
            grid=(b, s // CHUNK),
            in_specs=[
                pl.BlockSpec((None, CHUNK, in_width), lambda bi, n, g: (bi, n, 0)),
                tab_spec, tab_spec, tab_spec, tab_spec,
                head_spec, head_spec, head_spec,
                pl.BlockSpec((1, ret_width), lambda bi, n, g: (0, 0)),
                pl.BlockSpec((len(POOL_WINDOWS), pool_group, pool_group), lambda bi, n, g: (0, 0, 0)),
                pl.BlockSpec((1, pool_width), lambda bi, n, g: (0, 0)),
            ],
            out_specs=pl.BlockSpec((None, CHUNK, mix_width), lambda bi, n, g: (bi, n, 0)),
            scratch_shapes=[pltpu.VMEM((RET_HEADS, HEAD_DIM, HEAD_DIM), F32),
                            pltpu.VMEM((MAX_WINDOW, pool_width), F32)],
        ),
        compiler_params=pltpu.CompilerParams(
            dimension_semantics=("arbitrary", "arbitrary"),
            vmem_limit_bytes=VMEM_LIMIT_BYTES),
        name="mixer",
    )(gch, proj3, cq, sq, ck, sk, decay, xi, zeta, gn_gain, pool_w_bf16, pool_scale)


def _mixer_tables(seq):
    half = HEAD_DIM // 2
    inv = ROPE_BASE ** (-jnp.arange(half, dtype=F32) / half)
    ang = jnp.arange(seq).astype(F32)[:, None] * inv[None, :]
    cos, sin = jnp.cos(ang), jnp.sin(ang)
    cos2 = jnp.concatenate([cos, cos], axis=-1)
    sin2 = jnp.concatenate([-sin, sin], axis=-1)
    kscale = HEAD_DIM ** -0.5
    log_g = jnp.log(1.0 - 2.0 ** (-5.0 - jnp.arange(RET_HEADS, dtype=F32)))
    idx = jnp.arange(CHUNK, dtype=F32)
    diff = idx[:, None] - idx[None, :]
    decay = jnp.where(diff[None] >= 0, jnp.exp(diff[None] * log_g[:, None, None]), 0.0)
    zeta = jnp.exp((CHUNK - 1.0 - idx)[None, :] * log_g[:, None])
    xi = jnp.exp((idx + 1.0)[None, :] * log_g[:, None])
    g_chunk = jnp.exp(CHUNK * log_g)
    bshape = (RET_HEADS, CHUNK, HEAD_DIM)
    xi_b = jnp.broadcast_to(xi[:, :, None], bshape)
    zeta_b = jnp.broadcast_to(zeta[:, :, None], bshape)
    return cos2, sin2, cos2 * kscale, sin2 * kscale, decay, xi_b, zeta_b, g_chunk


def _outproj_kernel(mixed_ref, x_ref, wout_ref, g1_ref, b1_ref, rwt_ref, rb_ref,
                    h_ref, hrow_ref, idx_ref, gate_ref, rank_ref, cnt_ref, carry_scr):
    i = pl.program_id(0)

    @pl.when(i == 0)
    def _():
        carry_scr[...] = jnp.zeros_like(carry_scr)

    mix = jnp.dot(mixed_ref[...], wout_ref[...], preferred_element_type=F32)
    r = DEEPNORM_ALPHA * x_ref[...] + mix
    mu = jnp.mean(r, axis=-1, keepdims=True)
    rc = r - mu
    var = jnp.mean(rc * rc, axis=-1, keepdims=True)
    h = rc * lax.rsqrt(var + LN_EPS) * g1_ref[...] + b1_ref[...]
    h_ref[...] = h
    hrow_ref[...] = h_ref[...].reshape(hrow_ref.shape)

    logits = lax.dot_general(rwt_ref[...], h, (((1,), (1,)), ((), ())),
                             precision=lax.Precision.HIGHEST, preferred_element_type=F32) + rb_ref[...]
    ne, tm = logits.shape
    eiota = lax.broadcasted_iota(jnp.int32, (ne, tm), 0).astype(F32)
    vals, idxs = [], []
    l = logits
    for _ in range(TOP_K):
        m = jnp.max(l, axis=0, keepdims=True)
        ix = jnp.min(jnp.where(l == m, eiota, float(ne)), axis=0, keepdims=True)
        vals.append(m)
        idxs.append(ix)
        l = jnp.where(eiota == ix, -jnp.inf, l)
    ex = [jnp.exp(v - vals[0]) for v in vals]
    den = ex[0]
    for e in ex[1:]:
        den = den + e
    gate_ref[...] = jnp.concatenate([e / den for e in ex], axis=0)
    idx_ref[...] = jnp.concatenate(idxs, axis=0).astype(jnp.int32)

    multi = jnp.zeros((ne, tm), F32)
    for ix in idxs:
        multi = multi + (eiota == ix).astype(F32)
    ri = lax.broadcasted_iota(jnp.int32, (tm, tm), 0)
    ci = lax.broadcasted_iota(jnp.int32, (tm, tm), 1)
    upper = (ri < ci).astype(BF16)
    carry = carry_scr[...]
    prefix = jnp.dot(multi.astype(BF16), upper, preferred_element_type=F32) + carry
    ranks = [jnp.sum(jnp.where(eiota == ix, prefix, 0.0), axis=0, keepdims=True) for ix in idxs]
    rank_ref[...] = jnp.concatenate(ranks, axis=0).astype(jnp.int32)
    carry = carry + jnp.sum(multi, axis=1, keepdims=True)
    carry_scr[...] = carry
    cnt_ref[...] = jnp.broadcast_to(carry, cnt_ref.shape).astype(jnp.int32)


def _out_proj(mixed2d, x2d, wout_bf16, g1, b1, rwt, rb):
    t, d = x2d.shape
    mw = mixed2d.shape[1]
    ne = rwt.shape[0]
    tm = OUTPROJ_TM
    row = lambda i: (i, 0)
    fixed = lambda i: (0, 0)
    col = lambda i: (0, i)
    return pl.pallas_call(
        _outproj_kernel,
        out_shape=(jax.ShapeDtypeStruct((t, d), F32),
                   jax.ShapeDtypeStruct((t, 1, d), F32),
                   jax.ShapeDtypeStruct((TOP_K, t), jnp.int32),
                   jax.ShapeDtypeStruct((TOP_K, t), F32),
                   jax.ShapeDtypeStruct((TOP_K, t), jnp.int32),
                   jax.ShapeDtypeStruct((ne, 128), jnp.int32)),
        grid=(t // tm,),
        in_specs=[pl.BlockSpec((tm, mw), row), pl.BlockSpec((tm, d), row),
                  pl.BlockSpec((mw, d), fixed), pl.BlockSpec((1, d), fixed), pl.BlockSpec((1, d), fixed),
                  pl.BlockSpec((ne, d), fixed), pl.BlockSpec((ne, 1), fixed)],
        out_specs=(pl.BlockSpec((tm, d), row), pl.BlockSpec((tm, 1, d), lambda i: (i, 0, 0)),
                   pl.BlockSpec((TOP_K, tm), col),
                   pl.BlockSpec((TOP_K, tm), col), pl.BlockSpec((TOP_K, tm), col),
                   pl.BlockSpec((ne, 128), fixed)),
        scratch_shapes=[pltpu.VMEM((ne, 1), F32)],
        compiler_params=pltpu.CompilerParams(
            dimension_semantics=("arbitrary",),
            vmem_limit_bytes=VMEM_LIMIT_BYTES),
        name="out_proj_router",
    )(mixed2d, x2d, wout_bf16, g1, b1, rwt, rb)


def _slots_kernel(idx_ref, rank_ref, off_ref, tok_ref, pos_ref, *, n_tokens):
    phase = pl.program_id(0)
    c = pl.program_id(1)
    nc = pl.num_programs(1)
    slots_per = tok_ref.shape[0] // nc
    assign_per = idx_ref.shape[0] // nc

    @pl.when(phase == 0)
    def _():
        def init(i, carry):
            tok_ref[c * slots_per + i] = 0
            return carry
        lax.fori_loop(0, slots_per, init, 0, unroll=8)

    @pl.when(phase == 1)
    def _():
        def body(i, carry):
            a = c * assign_per + i
            p = off_ref[idx_ref[a]] + rank_ref[a]
            pos_ref[a] = p
            tok_ref[p] = lax.rem(a, n_tokens)
            return carry
        lax.fori_loop(0, assign_per, body, 0, unroll=8)


def _slots(idx_flat, rank_flat, off, n_slots, n_tokens):
    smem = pl.BlockSpec(memory_space=pltpu.SMEM)
    return pl.pallas_call(
        functools.partial(_slots_kernel, n_tokens=n_tokens),
        out_shape=(jax.ShapeDtypeStruct((n_slots,), jnp.int32),
                   jax.ShapeDtypeStruct(idx_flat.shape, jnp.int32)),
        grid=(2, SLOTS_CHUNKS),
        in_specs=[smem, smem, smem],
        out_specs=(smem, smem),
        compiler_params=pltpu.CompilerParams(dimension_semantics=("arbitrary", "arbitrary")),
        name="slots",
    )(idx_flat, rank_flat, off)


def _moe_kernel(ie_ref, ir0_ref, inr_ref, nw_ref, tok_ref,
                hrow_hbm, wgu_ref, bgu_ref, wd_ref, bd_ref, out_hbm,
                xrow_scr, xtmp_scr, xs_scr, acc_scr, wgu_bf, wdt_scr, wd_bf, orow_a, orow_b, gsem, osem):
    w = pl.program_id(0)
    j = pl.program_id(1)
    nj = pl.num_programs(1)
    nw = nw_ref[0]
    sub = MOE_SUB
    fu = MOE_FU
    hu = fu // 2
    d = xs_scr.shape[1]
    nsub = lax.div(inr_ref[w] + (sub - 1), sub)

    def sub_rows(s):
        return pl.ds(pl.multiple_of(s * sub, sub), sub)

    @pl.when(jnp.logical_and(w == 0, j == 0))
    def _():
        xrow_scr[...] = jnp.zeros_like(xrow_scr)

    @pl.when(jnp.logical_and(w < nw, j == 0))
    def _():
        acc_scr[...] = jnp.zeros_like(acc_scr)
        r0 = ir0_ref[w]
        nr = inr_ref[w]

        def row_copy(r):
            return pltpu.make_async_copy(hrow_hbm.at[tok_ref[r0 + r]], xrow_scr.at[r], gsem.at[0])

        def start(r, c):
            row_copy(r).start()
            return c
        lax.fori_loop(0, nr, start, 0)

        def wait(r, c):
            row_copy(r).wait()
            return c
        lax.fori_loop(0, nr, wait, 0)

        def conv(s, c):
            rows = sub_rows(s)
            xtmp_scr[...] = xrow_scr[rows].reshape(sub, d)
            xs_scr[rows, :] = xtmp_scr[...].astype(BF16)
            return c
        lax.fori_loop(0, nsub, conv, 0)

    @pl.when(w < nw)
    def _():
        wgu_bf[...] = wgu_ref[0].astype(BF16)
        for c in range(d // 128):
            cols = slice(c * 128, (c + 1) * 128)
            wdt_scr[c, pl.ds(0, hu, stride=2), :] = wd_ref[0, 0:hu, cols]
            wdt_scr[c, pl.ds(1, hu, stride=2), :] = wd_ref[0, hu:fu, cols]
            wd_bf[:, cols] = wdt_scr[c].astype(BF16)
        bgu = bgu_ref[0]
        lane = lax.broadcasted_iota(jnp.int32, (sub, fu), 1)
        even = (lane & 1) == 0

        def glu(z):
            zc = jnp.minimum(z, SWIGLU_LIMIT)
            return zc * jax.nn.sigmoid(SWIGLU_ALPHA * zc)

        def lin(z):
            return jnp.clip(z, -SWIGLU_LIMIT, SWIGLU_LIMIT) + 1.0

        def body(s, c):
            rows = sub_rows(s)
            hgu = jnp.dot(xs_scr[rows, :], wgu_bf[...], preferred_element_type=F32) + bgu
            ha = hgu[:, :fu]
            hb = hgu[:, fu:]
            pa = glu(ha) * pltpu.roll(lin(ha), fu - 1, 1)
            pb = pltpu.roll(glu(hb), 1, 1) * lin(hb)
            act = jnp.where(even, pa, pb).astype(BF16)
            acc_scr[rows, :] += jnp.dot(act, wd_bf[...], preferred_element_type=F32)
            return c
        lax.fori_loop(0, nsub, body, 0)

        @pl.when(j == nj - 1)
        def _():
            bd = bd_ref[0]
            r0 = ir0_ref[w]
            bufs = (orow_a, orow_b)

            def out_copy(s):
                row = pl.multiple_of(r0 + s * sub, sub)
                return pltpu.make_async_copy(bufs[s % 2], out_hbm.at[pl.ds(row, sub)], osem.at[s % 2])

            for s in range(MOE_NSUB):
                @pl.when(s < nsub)
                def _():
                    if s >= 2:
                        out_copy(s - 2).wait()
                    rows = pl.ds(s * sub, sub)
                    acc_scr[rows, :] += bd
                    bufs[s % 2][...] = acc_scr[rows, :].reshape(sub, 1, d)
                    out_copy(s).start()
            for s in range(MOE_NSUB):
                @pl.when(jnp.logical_and(s < nsub, s + 2 >= nsub))
                def _():
                    out_copy(s).wait()

            @pl.when(w == nw - 1)
            def _():
                used = r0 + nsub * sub
                ntail = lax.div(out_hbm.shape[0] - used, sub)
                orow_a[...] = jnp.zeros_like(orow_a)

                def tail_copy(s):
                    row = pl.multiple_of(used + s * sub, sub)
                    return pltpu.make_async_copy(orow_a, out_hbm.at[pl.ds(row, sub)], osem.at[0])

                def tstart(s, c):
                    tail_copy(s).start()
                    return c
                lax.fori_loop(0, ntail, tstart, 0)

                def twait(s, c):
                    tail_copy(s).wait()
                    return c
                lax.fori_loop(0, ntail, twait, 0)


def _moe(item_e, item_r0, item_nr, nw, tok_of_slot, hrow, w_gate_up, b_gate_up, w_down, b_down, n_slots):
    t, _, d = hrow.shape
    ne, _, two_f = w_gate_up.shape
    f = two_f // 2
    n_items = item_e.shape[0]
    fu = MOE_FU
    nj = f // fu

    def step(wi, j, nw_ref):
        return jnp.where(wi < nw_ref[0], j, nj - 1)

    grid_spec = pltpu.PrefetchScalarGridSpec(
        num_scalar_prefetch=5,
        grid=(n_items, nj),
        in_specs=[
            pl.BlockSpec(memory_space=pl.ANY),
            pl.BlockSpec((1, d, 2 * fu), lambda wi, j, ie, ir0, inr, nwr, tok: (ie[wi], 0, step(wi, j, nwr))),
            pl.BlockSpec((1, 1, 2 * fu), lambda wi, j, ie, ir0, inr, nwr, tok: (ie[wi], 0, step(wi, j, nwr))),
            pl.BlockSpec((1, fu, d), lambda wi, j, ie, ir0, inr, nwr, tok: (ie[wi], step(wi, j, nwr), 0)),
            pl.BlockSpec((1, 1, d), lambda wi, j, ie, ir0, inr, nwr, tok: (ie[wi], 0, 0)),
        ],
        out_specs=pl.BlockSpec(memory_space=pl.ANY),
        scratch_shapes=[
            pltpu.VMEM((MOE_RMAX, 1, d), F32),
            pltpu.VMEM((MOE_SUB, d), F32),
            pltpu.VMEM((MOE_RMAX, d), BF16),
            pltpu.VMEM((MOE_RMAX, d), F32),
            pltpu.VMEM((d, 2 * fu), BF16),
            pltpu.VMEM((d // 128, fu, 128), F32),
            pltpu.VMEM((fu, d), BF16),
            pltpu.VMEM((MOE_SUB, 1, d), F32),
            pltpu.VMEM((MOE_SUB, 1, d), F32),
            pltpu.SemaphoreType.DMA((1,)),
            pltpu.SemaphoreType.DMA((2,)),
        ],
    )
    return pl.pallas_call(
        _moe_kernel,
        out_shape=jax.ShapeDtypeStruct((n_slots, 1, d), F32),
        grid_spec=grid_spec,
        compiler_params=pltpu.CompilerParams(
            dimension_semantics=("arbitrary", "arbitrary"),
            vmem_limit_bytes=VMEM_LIMIT_BYTES),
        name="moe_experts",
    )(item_e, item_r0, item_nr, nw, tok_of_slot,
      hrow, w_gate_up, b_gate_up.reshape(ne, 1, two_f), w_down, b_down.reshape(ne, 1, d))


def _combine_kernel(pos_ref, rows_hbm, h_ref, gate_ref, g2_ref, b2_ref, o_ref,
                    rrow_a, rrow_b, rows_scr, sem, *, n_tokens):
    i = pl.program_id(0)
    n = pl.num_programs(0)
    tm, d = h_ref.shape
    bufs = (rrow_a, rrow_b)

    def row_copy(step, slot, k, r):
        p = pos_ref[k * n_tokens + step * tm + r]
        return pltpu.make_async_copy(rows_hbm.at[p], bufs[slot].at[k * tm + r], sem.at[slot])

    def issue(step, slot):
        for k in range(TOP_K):
            def body(r, c):
                row_copy(step, slot, k, r).start()
                return c
            lax.fori_loop(0, tm, body, 0)

    def drain(step, slot):
        for k in range(TOP_K):
            def body(r, c):
                row_copy(step, slot, k, r).wait()
                return c
            lax.fori_loop(0, tm, body, 0)

    @pl.when(i == 0)
    def _():
        issue(0, 0)

    for slot in range(2):
        @pl.when(jnp.logical_and(i + 1 < n, lax.rem(i + 1, 2) == slot))
        def _():
            issue(i + 1, slot)

    for slot in range(2):
        @pl.when(lax.rem(i, 2) == slot)
        def _():
            drain(i, slot)
            rows_scr[...] = bufs[slot][...].reshape(TOP_K * tm, d)

    y = gate_ref[:, 0:1] * rows_scr[0:tm, :]
    for k in range(1, TOP_K):
        y = y + gate_ref[:, k:k + 1] * rows_scr[k * tm:(k + 1) * tm, :]
    r = DEEPNORM_ALPHA * h_ref[...] + y
    mu = jnp.mean(r, axis=-1, keepdims=True)
    rc = r - mu
    var = jnp.mean(rc * rc, axis=-1, keepdims=True)
    o_ref[...] = rc * lax.rsqrt(var + LN_EPS) * g2_ref[...] + b2_ref[...]


def _combine(pos_flat, rows, h, gates_tk, g2, b2):
    t, d = h.shape
    tm = COMBINE_TM
    grid_spec = pltpu.PrefetchScalarGridSpec(
        num_scalar_prefetch=1,
        grid=(t // tm,),
        in_specs=[
            pl.BlockSpec(memory_space=pl.ANY),
            pl.BlockSpec((tm, d), lambda i, p: (i, 0)),
            pl.BlockSpec((tm, TOP_K), lambda i, p: (i, 0)),
            pl.BlockSpec((1, d), lambda i, p: (0, 0)),
            pl.BlockSpec((1, d), lambda i, p: (0, 0)),
        ],
        out_specs=pl.BlockSpec((tm, d), lambda i, p: (i, 0)),
        scratch_shapes=[pltpu.VMEM((TOP_K * tm, 1, d), F32), pltpu.VMEM((TOP_K * tm, 1, d), F32),
                        pltpu.VMEM((TOP_K * tm, d), F32), pltpu.SemaphoreType.DMA((2,))],
    )
    return pl.pallas_call(
        functools.partial(_combine_kernel, n_tokens=t),
        out_shape=jax.ShapeDtypeStruct((t, d), F32),
        grid_spec=grid_spec,
        compiler_params=pltpu.CompilerParams(
            dimension_semantics=("arbitrary",),
            vmem_limit_bytes=VMEM_LIMIT_BYTES),
        name="combine_ln2",
    )(pos_flat, rows, h, gates_tk, g2, b2)


def _work_items(counts, n_items):
    sub, rmax = MOE_SUB, MOE_RMAX
    padded = ((counts + sub - 1) // sub) * sub
    off = jnp.cumsum(padded) - padded
    per_e = (counts + rmax - 1) // rmax
    ends = jnp.cumsum(per_e)
    nw = ends[-1]
    wi = jnp.minimum(jnp.arange(n_items, dtype=jnp.int32), nw - 1)
    e = jnp.sum((ends[None, :] <= wi[:, None]).astype(jnp.int32), axis=1)
    local = wi - (ends - per_e)[e]
    r0 = off[e] + local * rmax
    nr = jnp.where(jnp.arange(n_items) < nw, jnp.minimum(rmax, counts[e] - local * rmax), 0)
    return (off.astype(jnp.int32), e.astype(jnp.int32), r0.astype(jnp.int32), nr.astype(jnp.int32),
            nw.reshape(1).astype(jnp.int32))


def kernel(x, w_in, ret_gn_gain, pool_w, pool_scale, w_out, ln1_g, ln1_b, router_w, router_b,
           w_gate_up, b_gate_up, w_down, b_down, ln2_g, ln2_b):
    b, s, d = x.shape
    t = b * s
    ret_width = ret_gn_gain.shape[-1]
    pool_width = pool_scale.shape[-1]
    tables = _mixer_tables(s)
    n_assign = t * TOP_K
    n_slots = n_assign + N_EXPERTS * MOE_SUB
    n_items = N_EXPERTS + -(-n_assign // MOE_RMAX)

    hcur = x.reshape(t, d)
    for l in range(DEPTH):
        proj = _in_proj(hcur, w_in[l].astype(BF16))
        mixed = _mixer(proj.reshape(b, s, -1), tables, ret_gn_gain[l].reshape(1, -1),
                       pool_w[l].astype(BF16), pool_scale[l].reshape(1, -1), ret_width, pool_width)
        h1, h1row, idx_kt, gate_kt, rank_kt, cnt = _out_proj(
            mixed.reshape(t, -1), hcur, w_out[l].astype(BF16), ln1_g[l].reshape(1, d), ln1_b[l].reshape(1, d),
            router_w[l].T, router_b[l].reshape(-1, 1))
        off, item_e, item_r0, item_nr, nw = _work_items(cnt[:, 0], n_items)
        tok_of_slot, pos = _slots(idx_kt.reshape(-1), rank_kt.reshape(-1), off, n_slots, t)
        rows = _moe(item_e, item_r0, item_nr, nw, tok_of_slot, h1row,
                    w_gate_up[l], b_gate_up[l], w_down[l], b_down[l], n_slots)
        hcur = _combine(pos, rows, h1, gate_kt.T, ln2_g[l].reshape(1, d), ln2_b[l].reshape(1, d))
    return hcur.reshape(b, s, d)
```
